```python
import math
import jax, jax.numpy as jnp
from jax import lax
import numpy as np

D_MODEL = 1024
BATCH = 16
SEQ = 4096
DEPTH = 1

ATT_HEADS = 8
ATT_KV_HEADS = 2
ATT_HEAD_DIM = 64
ROT_DIM = ATT_HEAD_DIM // 4
ROPE_THETA = 500000.0
WINDOW = 128
ATT_BLOCK = 128
HG_HEADS = 4
HG_DIM = 128
HG_CHUNK = 64
ATT_WIDTH = ATT_HEADS * ATT_HEAD_DIM
KV_WIDTH = ATT_KV_HEADS * ATT_HEAD_DIM
HG_WIDTH = HG_HEADS * HG_DIM
MIX_WIDTH = ATT_WIDTH + HG_WIDTH
IN_SIZES = (ATT_WIDTH, KV_WIDTH, KV_WIDTH, HG_WIDTH, HG_WIDTH, HG_WIDTH, HG_WIDTH, HG_WIDTH)
IN_WIDTH = ATT_WIDTH + 2 * KV_WIDTH + 5 * HG_WIDTH
D_FF = 2816
FFN_RES_WEIGHT = 0.5
EPS = 1e-6
NEG_INF = -1e30

kernel_name = "hymba_swa_hgrn2_macaron_encoder"


def rmsnorm(x, w):
    xf = x.astype(jnp.float32)
    y = xf * lax.rsqrt(jnp.mean(xf * xf, axis=-1, keepdims=True) + EPS)
    return (y * w.astype(jnp.float32)).astype(x.dtype)


def swiglu(x, w_gate, w_up, w_down):
    return (jax.nn.silu(x @ w_gate) * (x @ w_up)) @ w_down


def partial_rope(t, pos):
    half = ROT_DIM // 2
    inv_freq = jnp.exp(-math.log(ROPE_THETA) * jnp.arange(half, dtype=jnp.float32) * 2.0 / ROT_DIM)
    ang = pos[:, None] * inv_freq[None, :]
    cos = jnp.cos(ang)[None, :, None, :]
    sin = jnp.sin(ang)[None, :, None, :]
    tf = t.astype(jnp.float32)
    t1, t2, rest = tf[..., :half], tf[..., half:ROT_DIM], tf[..., ROT_DIM:]
    return jnp.concatenate([t1 * cos - t2 * sin, t2 * cos + t1 * sin, rest], axis=-1).astype(t.dtype)


def windowed_gqa_sink(q, k, v, sink):
    B, S = q.shape[0], q.shape[1]
    nb = S // ATT_BLOCK
    G = ATT_HEADS // ATT_KV_HEADS
    qb = q.reshape(B, nb, ATT_BLOCK, ATT_KV_HEADS, G, ATT_HEAD_DIM)
    pad = ((0, 0), (ATT_BLOCK, ATT_BLOCK), (0, 0), (0, 0))
    kp = jnp.pad(k, pad).reshape(B, nb + 2, ATT_BLOCK, ATT_KV_HEADS, ATT_HEAD_DIM)
    vp = jnp.pad(v, pad).reshape(B, nb + 2, ATT_BLOCK, ATT_KV_HEADS, ATT_HEAD_DIM)

    def band(t):
        return jnp.concatenate([t[:, :-2], t[:, 1:-1], t[:, 2:]], axis=2)

    kb, vb = band(kp), band(vp)
    blk = jnp.arange(nb)
    q_pos = blk[:, None, None] * ATT_BLOCK + jnp.arange(ATT_BLOCK)[None, :, None]
    k_pos = blk[:, None, None] * ATT_BLOCK - ATT_BLOCK + jnp.arange(3 * ATT_BLOCK)[None, None, :]
    valid = (k_pos >= 0) & (k_pos < S) & (jnp.abs(k_pos - q_pos) <= WINDOW)
    sink_f = sink.astype(jnp.float32).reshape(ATT_KV_HEADS, G)[None, :, :, None, None]
    scale = ATT_HEAD_DIM ** -0.5

    def one_block(args):
        qn, kn, vn, mn = args
        s = jnp.einsum('bqhgd,bkhd->bhgqk', qn, kn).astype(jnp.float32) * scale
        s = jnp.where(mn, s, NEG_INF)
        m = jnp.maximum(jnp.max(s, axis=-1, keepdims=True), sink_f)
        p = jnp.exp(s - m)
        p = p / (jnp.sum(p, axis=-1, keepdims=True) + jnp.exp(sink_f - m))
        return jnp.einsum('bhgqk,bkhd->bqhgd', p.astype(vn.dtype), vn)

    o = lax.map(one_block, (jnp.moveaxis(qb, 1, 0), jnp.moveaxis(kb, 1, 0), jnp.moveaxis(vb, 1, 0), valid))
    return jnp.moveaxis(o, 0, 1).reshape(B, S, ATT_WIDTH)


def gla_chunk_scan(q, k, v, log_f):
    B, S, H, DK = q.shape
    DV = v.shape[-1]
    n = S // HG_CHUNK

    def chunks(t):
        return t.reshape(B, n, HG_CHUNK, H, t.shape[-1]).transpose(1, 0, 3, 2, 4)

    mask = jnp.tril(jnp.ones((HG_CHUNK, HG_CHUNK), dtype=bool))[:, :, None]

    def step(state, xs):
        qc, kc, vc, gc = xs
        b = jnp.cumsum(gc, axis=2)
        decay = jnp.exp(jnp.where(mask, b[:, :, :, None, :] - b[:, :, None, :, :], NEG_INF))
        scores = jnp.einsum('bhtsk,bhsk->bhts', qc[:, :, :, None, :] * decay, kc)
        o = (jnp.einsum('bhts,bhsv->bhtv', scores, vc)
             + jnp.einsum('bhtk,bhkv->bhtv', qc * jnp.exp(b), state))
        b_last = b[:, :, -1:, :]
        state = (state * jnp.exp(b_last)[:, :, 0, :, None]
                 + jnp.einsum('bhsk,bhsv->bhkv', kc * jnp.exp(b_last - b), vc))
        return state, o

    state0 = jnp.zeros((B, H, DK, DV), jnp.float32)
    _, o = lax.scan(step, state0, (chunks(q), chunks(k), chunks(v), chunks(log_f)))
    return o.transpose(1, 0, 3, 2, 4).reshape(B, S, H, DV)


def hgrn2_direction(q, v, f_logit, lb):
    f = lb + (1.0 - lb) * jax.nn.sigmoid(f_logit)
    return gla_chunk_scan(q, 1.0 - f, v, jnp.log(f))


def layer_lower_bound(lb_raw, layer):
    return jnp.cumsum(jax.nn.softmax(lb_raw.astype(jnp.float32), axis=0), axis=0)[layer]


def hybrid_mixer(u, w_in, attn_sink, attn_out_norm, lb_fwd, lb_bwd, hg_out_norm, w_out, layer):
    B, S, _ = u.shape
    proj = u @ w_in
    cuts, acc = [], 0
    for size in IN_SIZES[:-1]:
        acc += size
        cuts.append(acc)
    q, k, v, hq, hf_fwd, hf_bwd, hi, hg = jnp.split(proj, cuts, axis=-1)

    pos = jnp.arange(S, dtype=jnp.float32)
    q = partial_rope(q.reshape(B, S, ATT_HEADS, ATT_HEAD_DIM), pos)
    k = partial_rope(k.reshape(B, S, ATT_KV_HEADS, ATT_HEAD_DIM), pos)
    v = v.reshape(B, S, ATT_KV_HEADS, ATT_HEAD_DIM)
    att = rmsnorm(windowed_gqa_sink(q, k, v, attn_sink), attn_out_norm)

    heads = lambda t: t.reshape(B, S, HG_HEADS, HG_DIM).astype(jnp.float32)
    qh = jax.nn.sigmoid(heads(hq))
    vh = jax.nn.silu(heads(hi))
    lbf = layer_lower_bound(lb_fwd, layer).reshape(HG_HEADS, HG_DIM)
    lbb = layer_lower_bound(lb_bwd, layer).reshape(HG_HEADS, HG_DIM)
    o_fwd = hgrn2_direction(qh, vh, heads(hf_fwd), lbf)
    flip = lambda t: jnp.flip(t, axis=1)
    o_bwd = flip(hgrn2_direction(flip(qh), flip(vh), flip(heads(hf_bwd)), lbb))
    o = rmsnorm(o_fwd + o_bwd, hg_out_norm) * jax.nn.silu(heads(hg))
    hgo = o.reshape(B, S, HG_WIDTH).astype(u.dtype)

    return jnp.concatenate([att, hgo], axis=-1) @ w_out


def setup_inputs(seed: int = 0) -> dict:
    key = jax.random.key(seed)
    ks = jax.random.split(key, 24)
    f32 = jnp.float32

    def w(k, shape, fan_in):
        return jax.random.normal(k, shape, f32) * (fan_in ** -0.5)

    def gain(k, shape):
        return 1.0 + 0.02 * jax.random.normal(k, shape, f32)

    return {
        "x": jax.random.normal(ks[0], (BATCH, SEQ, D_MODEL), f32),
        "ffn1_pre_norm": gain(ks[1], (DEPTH, D_MODEL)),
        "ffn1_post_norm": gain(ks[2], (DEPTH, D_MODEL)),
        "ffn1_w_gate": w(ks[3], (DEPTH, D_MODEL, D_FF), D_MODEL),
        "ffn1_w_up": w(ks[4], (DEPTH, D_MODEL, D_FF), D_MODEL),
        "ffn1_w_down": w(ks[5], (DEPTH, D_FF, D_MODEL), D_FF),
        "mix_pre_norm": gain(ks[6], (DEPTH, D_MODEL)),
        "mix_post_norm": gain(ks[7], (DEPTH, D_MODEL)),
        "w_in": w(ks[8], (DEPTH, D_MODEL, IN_WIDTH), D_MODEL),
        "attn_sink": 0.5 * jax.random.normal(ks[9], (DEPTH, ATT_HEADS), f32),
        "attn_out_norm": gain(ks[10], (DEPTH, ATT_WIDTH)),
        "hgrn_lb_fwd": 0.1 * jax.random.normal(ks[11], (DEPTH + 1, HG_WIDTH), f32),
        "hgrn_lb_bwd": 0.1 * jax.random.normal(ks[12], (DEPTH + 1, HG_WIDTH), f32),
        "hgrn_out_norm": gain(ks[13], (DEPTH, HG_DIM)),
        "w_out": w(ks[14], (DEPTH, MIX_WIDTH, D_MODEL), MIX_WIDTH),
        "ffn2_pre_norm": gain(ks[15], (DEPTH, D_MODEL)),
        "ffn2_post_norm": gain(ks[16], (DEPTH, D_MODEL)),
        "ffn2_w_gate": w(ks[17], (DEPTH, D_MODEL, D_FF), D_MODEL),
        "ffn2_w_up": w(ks[18], (DEPTH, D_MODEL, D_FF), D_MODEL),
        "ffn2_w_down": w(ks[19], (DEPTH, D_FF, D_MODEL), D_FF),
    }


def reference(x, ffn1_pre_norm, ffn1_post_norm, ffn1_w_gate, ffn1_w_up, ffn1_w_down,
              mix_pre_norm, mix_post_norm, w_in, attn_sink, attn_out_norm,
              hgrn_lb_fwd, hgrn_lb_bwd, hgrn_out_norm, w_out,
              ffn2_pre_norm, ffn2_post_norm, ffn2_w_gate, ffn2_w_up, ffn2_w_down):
    for l in range(DEPTH):
        h = swiglu(rmsnorm(x, ffn1_pre_norm[l]), ffn1_w_gate[l], ffn1_w_up[l], ffn1_w_down[l])
        x = x + FFN_RES_WEIGHT * rmsnorm(h, ffn1_post_norm[l])
        h = hybrid_mixer(rmsnorm(x, mix_pre_norm[l]), w_in[l], attn_sink[l], attn_out_norm[l],
                         hgrn_lb_fwd, hgrn_lb_bwd, hgrn_out_norm[l], w_out[l], l)
        x = x + rmsnorm(h, mix_post_norm[l])
        h = swiglu(rmsnorm(x, ffn2_pre_norm[l]), ffn2_w_gate[l], ffn2_w_up[l], ffn2_w_down[l])
        x = x + FFN_RES_WEIGHT * rmsnorm(h, ffn2_post_norm[l])
    return x
```

```python
import functools
import math

import jax
import jax.numpy as jnp
from jax import lax
from jax.experimental import pallas as pl
from jax.experimental.pallas import tpu as pltpu

D_MODEL = 1024
BATCH = 16
SEQ = 4096
ATT_HEADS = 8
ATT_KV_HEADS = 2
ATT_HEAD_DIM = 64
ROT_DIM = ATT_HEAD_DIM // 4
ROPE_THETA = 500000.0
WINDOW = 128
HG_HEADS = 4
HG_DIM = 128
ATT_WIDTH = ATT_HEADS * ATT_HEAD_DIM
KV_WIDTH = ATT_KV_HEADS * ATT_HEAD_DIM
HG_WIDTH = HG_HEADS * HG_DIM
IN_WIDTH = ATT_WIDTH + 2 * KV_WIDTH + 5 * HG_WIDTH
D_FF = 2816
FFN_RES_WEIGHT = 0.5
EPS = 1e-6
NEG_INF = -1e30

TOKENS = BATCH * SEQ
LANES = 128
ROW_TILE = 512
FF_CHUNK = 256
ATT_BLOCK = WINDOW
ATT_TILE = 512
HG_CHUNK = 64
HG_TILE = 512
VMEM_LIMIT = 48 * 1024 * 1024

F32 = jnp.float32
BF16 = jnp.bfloat16
NT_DIMS = (((1,), (1,)), ((), ()))
TN_DIMS = (((0,), (0,)), ((), ()))


def _rms(x, w):
    return x * lax.rsqrt(jnp.mean(x * x, axis=-1, keepdims=True) + EPS) * w


def _dot(a, b):
    return jnp.dot(a, b, preferred_element_type=F32)


def _resident(shape):
    return pl.BlockSpec(shape, lambda *_: (0,) * len(shape), pipeline_mode=pl.Buffered(1))


def _params(*semantics):
    return pltpu.CompilerParams(dimension_semantics=semantics, vmem_limit_bytes=VMEM_LIMIT)


def _ffn_body(x_ref, pre_ref, post_ref, wg_ref, wu_ref, wd_ref, o_ref):
    x = x_ref[...]
    h = _rms(x, pre_ref[...]).astype(BF16)
    acc = None
    for c in range(D_FF // FF_CHUNK):
        sl = slice(c * FF_CHUNK, (c + 1) * FF_CHUNK)
        g = _dot(h, wg_ref[:, sl])
        u = _dot(h, wu_ref[:, sl])
        a = (g * jax.nn.sigmoid(g) * u).astype(BF16)
        d = _dot(a, wd_ref[sl, :])
        acc = d if acc is None else acc + d
    o_ref[...] = x + FFN_RES_WEIGHT * _rms(acc, post_ref[...])


def _ffn(x, pre, post, wg, wu, wd):
    row = pl.BlockSpec((ROW_TILE, D_MODEL), lambda i: (i, 0))
    return pl.pallas_call(
        _ffn_body,
        out_shape=jax.ShapeDtypeStruct((TOKENS, D_MODEL), F32),
        grid=(TOKENS // ROW_TILE,),
        in_specs=[row, _resident((1, D_MODEL)), _resident((1, D_MODEL)),
                  _resident((D_MODEL, D_FF)), _resident((D_MODEL, D_FF)),
                  _resident((D_FF, D_MODEL))],
        out_specs=row,
        compiler_params=_params("parallel"),
        name="ffn",
    )(x, pre, post, wg, wu, wd)


def _lower_bound(lb_ref, layer):
    rows = [lb_ref[r:r + 1, :] for r in range(lb_ref.shape[0])]
    m = functools.reduce(jnp.maximum, rows)
    e = [jnp.exp(r - m) for r in rows]
    return sum(e[:layer + 1]) / sum(e)


def _rope(t, cos, sin_hi, sin_lo):
    return (t * cos + pltpu.roll(t, LANES - ROT_DIM // 2, 1) * sin_hi
            + pltpu.roll(t, ROT_DIM // 2, 1) * sin_lo)


def _swap_halves(t):
    return pltpu.roll(t, LANES // 2, 1)


def _inproj_body(x_ref, pre_ref, w_ref, lbf_ref, lbb_ref, cos_ref, shi_ref, slo_ref,
                 q_ref, k2_ref, v2_ref, qs_ref, gf_ref, kf_ref, gb_ref, kb_ref,
                 hv_ref, gate_ref):
    u = _rms(x_ref[...], pre_ref[...]).astype(BF16)
    cos, shi, slo = cos_ref[...], shi_ref[...], slo_ref[...]

    def proj(lo, width):
        return _dot(u, w_ref[:, lo:lo + width])

    scale = ATT_HEAD_DIM ** -0.5
    for s in range(ATT_WIDTH // LANES):
        q = _rope(proj(s * LANES, LANES), cos, shi, slo)
        q_ref[:, s * LANES:(s + 1) * LANES] = (q * scale).astype(BF16)
    off = ATT_WIDTH
    k = _rope(proj(off, KV_WIDTH), cos, shi, slo)
    k2_ref[:, :LANES] = k.astype(BF16)
    k2_ref[:, LANES:] = _swap_halves(k).astype(BF16)
    off += KV_WIDTH
    v = proj(off, KV_WIDTH)
    v2_ref[:, :LANES] = v.astype(BF16)
    v2_ref[:, LANES:] = _swap_halves(v).astype(BF16)
    off += KV_WIDTH
    qs_ref[...] = jax.nn.sigmoid(proj(off, HG_WIDTH)).astype(BF16)
    off += HG_WIDTH
    for lb_ref, g_ref, k_ref in ((lbf_ref, gf_ref, kf_ref), (lbb_ref, gb_ref, kb_ref)):
        lb = _lower_bound(lb_ref, 0)
        f = lb + (1.0 - lb) * jax.nn.sigmoid(proj(off, HG_WIDTH))
        g_ref[...] = jnp.log(f).astype(BF16)
        k_ref[...] = (1.0 - f).astype(BF16)
        off += HG_WIDTH
    hi = proj(off, HG_WIDTH)
    hv_ref[...] = (hi * jax.nn.sigmoid(hi)).astype(BF16)
    off += HG_WIDTH
    hg = proj(off, HG_WIDTH)
    gate_ref[...] = (hg * jax.nn.sigmoid(hg)).astype(BF16)


def _rope_tables():
    half = ROT_DIM // 2
    inv_freq = jnp.exp(-math.log(ROPE_THETA) * jnp.arange(half, dtype=F32) * 2.0 / ROT_DIM)
    ang = jnp.arange(SEQ, dtype=F32)[:, None] * inv_freq[None, :]
    cos, sin = jnp.cos(ang), jnp.sin(ang)
    rest = ATT_HEAD_DIM - ROT_DIM
    ones, zeros = jnp.ones((SEQ, rest), F32), jnp.zeros((SEQ, rest), F32)
    zhalf = jnp.zeros((SEQ, half), F32)
    cos_t = jnp.concatenate([cos, cos, ones], axis=1)
    sin_hi = jnp.concatenate([-sin, zhalf, zeros], axis=1)
    sin_lo = jnp.concatenate([zhalf, sin, zeros], axis=1)
    return tuple(jnp.concatenate([t, t], axis=1) for t in (cos_t, sin_hi, sin_lo))


def _inproj(x, pre, w_in, lb_fwd, lb_bwd):
    tiles_per_seq = SEQ // ROW_TILE
    row = lambda width: pl.BlockSpec((ROW_TILE, width), lambda i: (i, 0))
    table = pl.BlockSpec((ROW_TILE, LANES), lambda i: (i % tiles_per_seq, 0))
    widths = (ATT_WIDTH, 2 * KV_WIDTH, 2 * KV_WIDTH) + (HG_WIDTH,) * 7
    return pl.pallas_call(
        _inproj_body,
        out_shape=[jax.ShapeDtypeStruct((TOKENS, w), BF16) for w in widths],
        grid=(TOKENS // ROW_TILE,),
        in_specs=[row(D_MODEL), _resident((1, D_MODEL)), _resident((D_MODEL, IN_WIDTH)),
                  _resident(lb_fwd.shape), _resident(lb_bwd.shape), table, table, table],
        out_specs=[row(w) for w in widths],
        compiler_params=_params("parallel"),
        name="inproj",
    )(x, pre, w_in, lb_fwd, lb_bwd, *_rope_tables())


def _attn_body(sink_ref, q_ref, kp_ref, kc_ref, kn_ref, vp_ref, vc_ref, vn_ref, wn_ref, o_ref):
    tile = pl.program_id(1)
    blocks_per_tile = ATT_TILE // ATT_BLOCK
    last_block = SEQ // ATT_BLOCK - 1
    qi = lax.broadcasted_iota(jnp.int32, (ATT_BLOCK, ATT_BLOCK), 0)
    kj = lax.broadcasted_iota(jnp.int32, (ATT_BLOCK, ATT_BLOCK), 1)
    left = lax.broadcasted_iota(jnp.int32, (1, LANES), 1) < ATT_HEAD_DIM
    wn = wn_ref[...]

    def piece(p_ref, c_ref, n_ref, j, which):
        idx = j + which - 1
        if idx < 0:
            return p_ref[0]
        if idx >= blocks_per_tile:
            return n_ref[0]
        return c_ref[0, idx * ATT_BLOCK:(idx + 1) * ATT_BLOCK, :]

    def split(pieces, g):
        zero = jnp.zeros((), BF16)
        a_src = slice(0, LANES) if g == 0 else slice(LANES, 2 * LANES)
        b_src = slice(LANES, 2 * LANES) if g == 0 else slice(0, LANES)
        a = [jnp.where(left, p[:, a_src], zero) for p in pieces]
        b = [jnp.where(left, zero, p[:, b_src]) for p in pieces]
        return jnp.concatenate(a + b, axis=0)

    for j in range(blocks_per_tile):
        block = tile * blocks_per_tile + j
        kpieces = [piece(kp_ref, kc_ref, kn_ref, j, w) for w in range(3)]
        vpieces = [piece(vp_ref, vc_ref, vn_ref, j, w) for w in range(3)]
        prev_ok = kj >= qi + jnp.where(block > 0, 0, ATT_BLOCK)
        next_ok = kj <= qi - jnp.where(block < last_block, 0, ATT_BLOCK)
        band = 3 * ATT_BLOCK

        def masked(sc):
            return jnp.concatenate(
                [jnp.where(prev_ok, sc[:, :ATT_BLOCK], NEG_INF), sc[:, ATT_BLOCK:2 * ATT_BLOCK],
                 jnp.where(next_ok, sc[:, 2 * ATT_BLOCK:], NEG_INF)], axis=1)

        outs = []
        for g in range(ATT_KV_HEADS):
            k_ab = split(kpieces, g)
            v_ab = split(vpieces, g)
            for pair in range(2):
                slab = 2 * g + pair
                qp = q_ref[0, j * ATT_BLOCK:(j + 1) * ATT_BLOCK, slab * LANES:(slab + 1) * LANES]
                s = lax.dot_general(qp, k_ab, NT_DIMS, preferred_element_type=F32)
                probs, inv = [], []
                for side in range(2):
                    sink = sink_ref[2 * slab + side]
                    sh = masked(s[:, side * band:(side + 1) * band])
                    m = jnp.maximum(jnp.max(sh, axis=-1, keepdims=True), sink)
                    p = jnp.exp(sh - m)
                    inv.append(1.0 / (jnp.sum(p, axis=-1, keepdims=True) + jnp.exp(sink - m)))
                    probs.append(p.astype(BF16))
                o = _dot(jnp.concatenate(probs, axis=1), v_ab)
                outs.append(o * jnp.where(left, inv[0], inv[1]))
        att = _rms(jnp.concatenate(outs, axis=1), wn)
        o_ref[0, j * ATT_BLOCK:(j + 1) * ATT_BLOCK, :] = att.astype(BF16)


def _attn(sink, q, k2, v2, wn):
    q3 = q.reshape(BATCH, SEQ, ATT_WIDTH)
    k3 = k2.reshape(BATCH, SEQ, 2 * KV_WIDTH)
    v3 = v2.reshape(BATCH, SEQ, 2 * KV_WIDTH)
    ratio = ATT_TILE // ATT_BLOCK
    last = SEQ // ATT_BLOCK - 1
    own = pl.BlockSpec((1, ATT_TILE, 2 * KV_WIDTH), lambda b, i: (b, i, 0))
    prev = pl.BlockSpec((1, ATT_BLOCK, 2 * KV_WIDTH),
                        lambda b, i: (b, jnp.maximum(i * ratio - 1, 0), 0))
    nxt = pl.BlockSpec((1, ATT_BLOCK, 2 * KV_WIDTH),
                       lambda b, i: (b, jnp.minimum((i + 1) * ratio, last), 0))
    qspec = pl.BlockSpec((1, ATT_TILE, ATT_WIDTH), lambda b, i: (b, i, 0))
    out = pl.pallas_call(
        _attn_body,
        out_shape=jax.ShapeDtypeStruct((BATCH, SEQ, ATT_WIDTH), BF16),
        grid=(BATCH, SEQ // ATT_TILE),
        in_specs=[pl.BlockSpec(memory_space=pltpu.SMEM), qspec, prev, own, nxt, prev, own, nxt,
                  _resident((1, ATT_WIDTH))],
        out_specs=qspec,
        compiler_params=_params("parallel", "parallel"),
        name="attn",
    )(sink, q3, k3, k3, k3, v3, v3, v3, wn)
    return out.reshape(TOKENS, ATT_WIDTH)


def _hgrn_chunk(qs_ref, hv_ref, g_ref, k_ref, o_ref, st_ref, start, cum, mask, last, mid):
    rows = pl.ds(start, HG_CHUNK)
    b = _dot(cum, g_ref[0, rows, :])
    b_last = b[last:last + 1, :]
    b_mid = b[mid:mid + 1, :]
    qs = qs_ref[0, rows, :].astype(F32)
    kk = k_ref[0, rows, :].astype(F32)
    v = hv_ref[0, rows, :]
    q_in = (qs * jnp.exp(b)).astype(BF16)
    k_out = (kk * jnp.exp(b_last - b)).astype(BF16)
    q_mid = (qs * jnp.exp(b - b_mid)).astype(BF16)
    k_mid = (kk * jnp.exp(b_mid - b)).astype(BF16)
    decay = jnp.exp(b_last)
    outs = []
    for h in range(HG_HEADS):
        sl = slice(h * HG_DIM, (h + 1) * HG_DIM)
        s = lax.dot_general(q_mid[:, sl], k_mid[:, sl], NT_DIMS, preferred_element_type=F32)
        s = jnp.where(mask, s, 0.0).astype(BF16)
        st = st_ref[h]
        o = _dot(s, v[:, sl]) + lax.dot_general(q_in[:, sl], st.astype(BF16), NT_DIMS,
                                                preferred_element_type=F32)
        outs.append(o)
        st_ref[h] = st * decay[:, sl] + lax.dot_general(v[:, sl], k_out[:, sl], TN_DIMS,
                                                       preferred_element_type=F32)
    o_ref[0, rows, :] = jnp.concatenate(outs, axis=1).astype(BF16)


def _hgrn_body(qsf_ref, hvf_ref, gf_ref, kf_ref, qsb_ref, hvb_ref, gb_ref, kb_ref,
               of_ref, ob_ref, stf_ref, stb_ref):
    @pl.when(pl.program_id(1) == 0)
    def _():
        stf_ref[...] = jnp.zeros_like(stf_ref)
        stb_ref[...] = jnp.zeros_like(stb_ref)

    t = lax.broadcasted_iota(jnp.int32, (HG_CHUNK, HG_CHUNK), 0)
    s = lax.broadcasted_iota(jnp.int32, (HG_CHUNK, HG_CHUNK), 1)
    causal, anticausal = s <= t, s >= t
    cum_f = jnp.where(causal, 1.0, 0.0).astype(BF16)
    cum_b = jnp.where(anticausal, 1.0, 0.0).astype(BF16)
    chunks = HG_TILE // HG_CHUNK

    def step(i, carry):
        _hgrn_chunk(qsf_ref, hvf_ref, gf_ref, kf_ref, of_ref, stf_ref,
                    pl.multiple_of(i * HG_CHUNK, HG_CHUNK), cum_f, causal,
                    HG_CHUNK - 1, HG_CHUNK // 2 - 1)
        _hgrn_chunk(qsb_ref, hvb_ref, gb_ref, kb_ref, ob_ref, stb_ref,
                    pl.multiple_of((chunks - 1 - i) * HG_CHUNK, HG_CHUNK), cum_b, anticausal,
                    0, HG_CHUNK // 2)
        return carry

    lax.fori_loop(0, chunks, step, 0)


def _hgrn(qs, hv, gf, kf, gb, kb):
    tiles = SEQ // HG_TILE
    r3 = lambda a: a.reshape(BATCH, SEQ, HG_WIDTH)
    fwd = pl.BlockSpec((1, HG_TILE, HG_WIDTH), lambda b, i: (b, i, 0))
    bwd = pl.BlockSpec((1, HG_TILE, HG_WIDTH), lambda b, i: (b, tiles - 1 - i, 0))
    state = pltpu.VMEM((HG_HEADS, HG_DIM, HG_DIM), F32)
    of, ob = pl.pallas_call(
        _hgrn_body,
        out_shape=[jax.ShapeDtypeStruct((BATCH, SEQ, HG_WIDTH), BF16)] * 2,
        grid=(BATCH, tiles),
        in_specs=[fwd, fwd, fwd, fwd, bwd, bwd, bwd, bwd],
        out_specs=[fwd, bwd],
        scratch_shapes=[state, state],
        compiler_params=_params("parallel", "arbitrary"),
        name="hgrn",
    )(r3(qs), r3(hv), r3(gf), r3(kf), r3(qs), r3(hv), r3(gb), r3(kb))
    return of.reshape(TOKENS, HG_WIDTH), ob.reshape(TOKENS, HG_WIDTH)


def _outproj_body(x_ref, att_ref, of_ref, ob_ref, gate_ref, wh_ref, wo_ref, post_ref, o_ref):
    o = of_ref[...].astype(F32) + ob_ref[...].astype(F32)
    gate = gate_ref[...].astype(F32)
    wh = wh_ref[...]
    heads = []
    for h in range(HG_HEADS):
        sl = slice(h * HG_DIM, (h + 1) * HG_DIM)
        heads.append((_rms(o[:, sl], wh) * gate[:, sl]).astype(BF16))
    y = _dot(att_ref[...], wo_ref[:ATT_WIDTH, :])
    for h in range(HG_HEADS):
        lo = ATT_WIDTH + h * HG_DIM
        y = y + _dot(heads[h], wo_ref[lo:lo + HG_DIM, :])
    o_ref[...] = x_ref[...] + _rms(y, post_ref[...])


def _outproj(x, att, of, ob, gate, wh, wo, post):
    row = lambda width: pl.BlockSpec((ROW_TILE, width), lambda i: (i, 0))
    return pl.pallas_call(
        _outproj_body,
        out_shape=jax.ShapeDtypeStruct((TOKENS, D_MODEL), F32),
        grid=(TOKENS // ROW_TILE,),
        in_specs=[row(D_MODEL), row(ATT_WIDTH), row(HG_WIDTH), row(HG_WIDTH), row(HG_WIDTH),
                  _resident((1, HG_DIM)), _resident((ATT_WIDTH + HG_WIDTH, D_MODEL)),
                  _resident((1, D_MODEL))],
        out_specs=row(D_MODEL),
        compiler_params=_params("parallel"),
        name="outproj",
    )(x, att, of, ob, gate, wh, wo, post)


def kernel(x, ffn1_pre_norm, ffn1_post_norm, ffn1_w_gate, ffn1_w_up, ffn1_w_down, mix_pre_norm, mix_post_norm, w_in, attn_sink, attn_out_norm, hgrn_lb_fwd, hgrn_lb_bwd, hgrn_out_norm, w_out, ffn2_pre_norm, ffn2_post_norm, ffn2_w_gate, ffn2_w_up, ffn2_w_down):
    assert x.shape == (BATCH, SEQ, D_MODEL) and ffn1_pre_norm.shape[0] == 1
    h = x.reshape(TOKENS, D_MODEL)
    bf = lambda w: w[0].astype(BF16)
    h = _ffn(h, ffn1_pre_norm, ffn1_post_norm, bf(ffn1_w_gate), bf(ffn1_w_up), bf(ffn1_w_down))
    q, k2, v2, qs, gf, kf, gb, kb, hv, gate = _inproj(
        h, mix_pre_norm, bf(w_in), hgrn_lb_fwd, hgrn_lb_bwd)
    att = _attn(attn_sink[0], q, k2, v2, attn_out_norm)
    of, ob = _hgrn(qs, hv, gf, kf, gb, kb)
    h = _outproj(h, att, of, ob, gate, hgrn_out_norm, bf(w_out), mix_post_norm)
    h = _ffn(h, ffn2_pre_norm, ffn2_post_norm, bf(ffn2_w_gate), bf(ffn2_w_up), bf(ffn2_w_down))
    return h.reshape(BATCH, SEQ, D_MODEL)
```

```python
import functools
import math

import jax
import jax.numpy as jnp
from jax import lax
from jax.experimental import pallas as pl
from jax.experimental.pallas import tpu as pltpu

D_MODEL = 1024
BATCH = 16
SEQ = 4096
ATT_HEADS = 8
ATT_KV_HEADS = 2
ATT_HEAD_DIM = 64
ROT_DIM = ATT_HEAD_DIM // 4
ROPE_THETA = 500000.0
WINDOW = 128
HG_HEADS = 4
HG_DIM = 128
ATT_WIDTH = ATT_HEADS * ATT_HEAD_DIM
KV_WIDTH = ATT_KV_HEADS * ATT_HEAD_DIM
HG_WIDTH = HG_HEADS * HG_DIM
IN_WIDTH = ATT_WIDTH + 2 * KV_WIDTH + 5 * HG_WIDTH
D_FF = 2816
FFN_RES_WEIGHT = 0.5
EPS = 1e-6
NEG_INF = -1e30

TOKENS = BATCH * SEQ
LANES = 128
ROW_TILE = 512
FF_CHUNK = 256
ATT_BLOCK = WINDOW
ATT_TILE = 512
HG_CHUNK = 128
HG_SUB = 32
HG_REFS = 2 * (HG_CHUNK // HG_SUB)
LOG2E = math.log2(math.e)
HG_TILE = 512
VMEM_LIMIT = 48 * 1024 * 1024

F32 = jnp.float32
BF16 = jnp.bfloat16
NT_DIMS = (((1,), (1,)), ((), ()))
TN_DIMS = (((0,), (0,)), ((), ()))


def _rms(x, w):
    return x * lax.rsqrt(jnp.mean(x * x, axis=-1, keepdims=True) + EPS) * w


def _dot(a, b):
    return jnp.dot(a, b, preferred_element_type=F32)


def _resident(shape):
    return pl.BlockSpec(shape, lambda *_: (0,) * len(shape), pipeline_mode=pl.Buffered(1))


def _params(*semantics):
    return pltpu.CompilerParams(dimension_semantics=semantics, vmem_limit_bytes=VMEM_LIMIT)


def _ffn_body(x_ref, pre_ref, post_ref, wg_ref, wu_ref, wd_ref, o_ref):
    x = x_ref[...]
    h = _rms(x, pre_ref[...]).astype(BF16)
    acc = None
    for c in range(D_FF // FF_CHUNK):
        sl = slice(c * FF_CHUNK, (c + 1) * FF_CHUNK)
        g = _dot(h, wg_ref[:, sl])
        u = _dot(h, wu_ref[:, sl])
        a = (g * jax.nn.sigmoid(g) * u).astype(BF16)
        d = _dot(a, wd_ref[sl, :])
        acc = d if acc is None else acc + d
    o_ref[...] = x + FFN_RES_WEIGHT * _rms(acc, post_ref[...])


def _ffn(x, pre, post, wg, wu, wd):
    row = pl.BlockSpec((ROW_TILE, D_MODEL), lambda i: (i, 0))
    return pl.pallas_call(
        _ffn_body,
        out_shape=jax.ShapeDtypeStruct((TOKENS, D_MODEL), F32),
        grid=(TOKENS // ROW_TILE,),
        in_specs=[row, _resident((1, D_MODEL)), _resident((1, D_MODEL)),
                  _resident((D_MODEL, D_FF)), _resident((D_MODEL, D_FF)),
                  _resident((D_FF, D_MODEL))],
        out_specs=row,
        compiler_params=_params("parallel"),
        name="ffn",
    )(x, pre, post, wg, wu, wd)


def _lower_bound(lb_ref, layer):
    rows = [lb_ref[r:r + 1, :] for r in range(lb_ref.shape[0])]
    m = functools.reduce(jnp.maximum, rows)
    e = [jnp.exp(r - m) for r in rows]
    return sum(e[:layer + 1]) / sum(e)


def _rope(t, cos, sin_hi, sin_lo):
    return (t * cos + pltpu.roll(t, LANES - ROT_DIM // 2, 1) * sin_hi
            + pltpu.roll(t, ROT_DIM // 2, 1) * sin_lo)


def _swap_halves(t):
    return pltpu.roll(t, LANES // 2, 1)


def _inproj_body(x_ref, pre_ref, w_ref, lbf_ref, lbb_ref, cos_ref, shi_ref, slo_ref,
                 q_ref, k2_ref, v2_ref, lqf_ref, lkf_ref, rf_ref, lqb_ref, lkb_ref, rb_ref,
                 hv_ref, gate_ref):
    u = _rms(x_ref[...], pre_ref[...]).astype(BF16)
    cos, shi, slo = cos_ref[...], shi_ref[...], slo_ref[...]

    def proj(lo, width):
        return _dot(u, w_ref[:, lo:lo + width])

    scale = ATT_HEAD_DIM ** -0.5
    for s in range(ATT_WIDTH // LANES):
        q = _rope(proj(s * LANES, LANES), cos, shi, slo)
        q_ref[:, s * LANES:(s + 1) * LANES] = (q * scale).astype(BF16)
    off = ATT_WIDTH
    k = _rope(proj(off, KV_WIDTH), cos, shi, slo)
    k2_ref[:, :LANES] = k.astype(BF16)
    k2_ref[:, LANES:] = _swap_halves(k).astype(BF16)
    off += KV_WIDTH
    v = proj(off, KV_WIDTH)
    v2_ref[:, :LANES] = v.astype(BF16)
    v2_ref[:, LANES:] = _swap_halves(v).astype(BF16)
    off += KV_WIDTH
    qs = jax.nn.sigmoid(proj(off, HG_WIDTH))
    off += HG_WIDTH
    t = lax.broadcasted_iota(jnp.int32, (HG_CHUNK, HG_CHUNK), 0)
    s = lax.broadcasted_iota(jnp.int32, (HG_CHUNK, HG_CHUNK), 1)
    prefix = jnp.where(s <= t, 1.0, 0.0).astype(BF16)
    suffix = jnp.where(s >= t, 1.0, 0.0).astype(BF16)
    for lb_ref, lq_ref, lk_ref, ref_ref, cum, reverse in (
            (lbf_ref, lqf_ref, lkf_ref, rf_ref, prefix, False),
            (lbb_ref, lqb_ref, lkb_ref, rb_ref, suffix, True)):
        lb = _lower_bound(lb_ref, 0)
        f = lb + (1.0 - lb) * jax.nn.sigmoid(proj(off, HG_WIDTH))
        g = (jnp.log(f) * LOG2E).astype(BF16)
        kk = 1.0 - f
        blocks, exits, mids = _hgrn_plan(reverse)
        for c in range(ROW_TILE // HG_CHUNK):
            lo = c * HG_CHUNK
            b = _dot(cum, g[lo:lo + HG_CHUNK])
            x = [b[r:r + 1, :] for r in exits]
            ref_ref[c * HG_REFS:(c + 1) * HG_REFS, :] = jnp.concatenate(
                x + [b[r:r + 1, :] for r in mids], axis=0)
            for k, blk in enumerate(blocks):
                rows = slice(lo + blk.start, lo + blk.stop)
                entry = b[blk] - x[k - 1] if k else b[blk]
                lq_ref[rows, :] = (qs[rows] * jnp.exp2(entry)).astype(BF16)
                lk_ref[rows, :] = (kk[rows] * jnp.exp2(x[k] - b[blk])).astype(BF16)
        off += HG_WIDTH
    hi = proj(off, HG_WIDTH)
    hv_ref[...] = (hi * jax.nn.sigmoid(hi)).astype(BF16)
    off += HG_WIDTH
    hg = proj(off, HG_WIDTH)
    gate_ref[...] = (hg * jax.nn.sigmoid(hg)).astype(BF16)


def _rope_tables():
    half = ROT_DIM // 2
    inv_freq = jnp.exp(-math.log(ROPE_THETA) * jnp.arange(half, dtype=F32) * 2.0 / ROT_DIM)
    ang = jnp.arange(SEQ, dtype=F32)[:, None] * inv_freq[None, :]
    cos, sin = jnp.cos(ang), jnp.sin(ang)
    rest = ATT_HEAD_DIM - ROT_DIM
    ones, zeros = jnp.ones((SEQ, rest), F32), jnp.zeros((SEQ, rest), F32)
    zhalf = jnp.zeros((SEQ, half), F32)
    cos_t = jnp.concatenate([cos, cos, ones], axis=1)
    sin_hi = jnp.concatenate([-sin, zhalf, zeros], axis=1)
    sin_lo = jnp.concatenate([zhalf, sin, zeros], axis=1)
    return tuple(jnp.concatenate([t, t], axis=1) for t in (cos_t, sin_hi, sin_lo))


def _inproj(x, pre, w_in, lb_fwd, lb_bwd):
    tiles_per_seq = SEQ // ROW_TILE
    row = lambda width: pl.BlockSpec((ROW_TILE, width), lambda i: (i, 0))
    table = pl.BlockSpec((ROW_TILE, LANES), lambda i: (i % tiles_per_seq, 0))
    ref_rows = ROW_TILE // HG_CHUNK * HG_REFS
    refs = pl.BlockSpec((ref_rows, HG_WIDTH), lambda i: (i, 0))
    tok = lambda width: (jax.ShapeDtypeStruct((TOKENS, width), BF16), row(width))
    ref = (jax.ShapeDtypeStruct((TOKENS // HG_CHUNK * HG_REFS, HG_WIDTH), F32), refs)
    outs = (tok(ATT_WIDTH), tok(2 * KV_WIDTH), tok(2 * KV_WIDTH),
            tok(HG_WIDTH), tok(HG_WIDTH), ref, tok(HG_WIDTH), tok(HG_WIDTH), ref,
            tok(HG_WIDTH), tok(HG_WIDTH))
    return pl.pallas_call(
        _inproj_body,
        out_shape=[shape for shape, _ in outs],
        grid=(TOKENS // ROW_TILE,),
        in_specs=[row(D_MODEL), _resident((1, D_MODEL)), _resident((D_MODEL, IN_WIDTH)),
                  _resident(lb_fwd.shape), _resident(lb_bwd.shape), table, table, table],
        out_specs=[spec for _, spec in outs],
        compiler_params=_params("parallel"),
        name="inproj",
    )(x, pre, w_in, lb_fwd, lb_bwd, *_rope_tables())


def _attn_body(sink_ref, q_ref, kp_ref, kc_ref, kn_ref, vp_ref, vc_ref, vn_ref, wn_ref, o_ref):
    tile = pl.program_id(1)
    blocks_per_tile = ATT_TILE // ATT_BLOCK
    last_block = SEQ // ATT_BLOCK - 1
    qi = lax.broadcasted_iota(jnp.int32, (ATT_BLOCK, ATT_BLOCK), 0)
    kj = lax.broadcasted_iota(jnp.int32, (ATT_BLOCK, ATT_BLOCK), 1)
    left = lax.broadcasted_iota(jnp.int32, (1, LANES), 1) < ATT_HEAD_DIM
    wn = wn_ref[...]

    def piece(p_ref, c_ref, n_ref, j, which):
        idx = j + which - 1
        if idx < 0:
            return p_ref[0]
        if idx >= blocks_per_tile:
            return n_ref[0]
        return c_ref[0, idx * ATT_BLOCK:(idx + 1) * ATT_BLOCK, :]

    def split(pieces, g):
        zero = jnp.zeros((), BF16)
        a_src = slice(0, LANES) if g == 0 else slice(LANES, 2 * LANES)
        b_src = slice(LANES, 2 * LANES) if g == 0 else slice(0, LANES)
        a = [jnp.where(left, p[:, a_src], zero) for p in pieces]
        b = [jnp.where(left, zero, p[:, b_src]) for p in pieces]
        return jnp.concatenate(a + b, axis=0)

    for j in range(blocks_per_tile):
        block = tile * blocks_per_tile + j
        kpieces = [piece(kp_ref, kc_ref, kn_ref, j, w) for w in range(3)]
        vpieces = [piece(vp_ref, vc_ref, vn_ref, j, w) for w in range(3)]
        prev_ok = kj >= qi + jnp.where(block > 0, 0, ATT_BLOCK)
        next_ok = kj <= qi - jnp.where(block < last_block, 0, ATT_BLOCK)
        band = 3 * ATT_BLOCK

        def masked(sc):
            return jnp.concatenate(
                [jnp.where(prev_ok, sc[:, :ATT_BLOCK], NEG_INF), sc[:, ATT_BLOCK:2 * ATT_BLOCK],
                 jnp.where(next_ok, sc[:, 2 * ATT_BLOCK:], NEG_INF)], axis=1)

        outs = []
        for g in range(ATT_KV_HEADS):
            k_ab = split(kpieces, g)
            v_ab = split(vpieces, g)
            for pair in range(2):
                slab = 2 * g + pair
                qp = q_ref[0, j * ATT_BLOCK:(j + 1) * ATT_BLOCK, slab * LANES:(slab + 1) * LANES]
                s = lax.dot_general(qp, k_ab, NT_DIMS, preferred_element_type=F32)
                probs, inv = [], []
                for side in range(2):
                    sink = sink_ref[2 * slab + side]
                    sh = masked(s[:, side * band:(side + 1) * band])
                    m = jnp.maximum(jnp.max(sh, axis=-1, keepdims=True), sink)
                    p = jnp.exp(sh - m)
                    inv.append(1.0 / (jnp.sum(p, axis=-1, keepdims=True) + jnp.exp(sink - m)))
                    probs.append(p.astype(BF16))
                o = _dot(jnp.concatenate(probs, axis=1), v_ab)
                outs.append(o * jnp.where(left, inv[0], inv[1]))
        att = _rms(jnp.concatenate(outs, axis=1), wn)
        o_ref[0, j * ATT_BLOCK:(j + 1) * ATT_BLOCK, :] = att.astype(BF16)


def _attn(sink, q, k2, v2, wn):
    q3 = q.reshape(BATCH, SEQ, ATT_WIDTH)
    k3 = k2.reshape(BATCH, SEQ, 2 * KV_WIDTH)
    v3 = v2.reshape(BATCH, SEQ, 2 * KV_WIDTH)
    ratio = ATT_TILE // ATT_BLOCK
    last = SEQ // ATT_BLOCK - 1
    own = pl.BlockSpec((1, ATT_TILE, 2 * KV_WIDTH), lambda b, i: (b, i, 0))
    prev = pl.BlockSpec((1, ATT_BLOCK, 2 * KV_WIDTH),
                        lambda b, i: (b, jnp.maximum(i * ratio - 1, 0), 0))
    nxt = pl.BlockSpec((1, ATT_BLOCK, 2 * KV_WIDTH),
                       lambda b, i: (b, jnp.minimum((i + 1) * ratio, last), 0))
    qspec = pl.BlockSpec((1, ATT_TILE, ATT_WIDTH), lambda b, i: (b, i, 0))
    out = pl.pallas_call(
        _attn_body,
        out_shape=jax.ShapeDtypeStruct((BATCH, SEQ, ATT_WIDTH), BF16),
        grid=(BATCH, SEQ // ATT_TILE),
        in_specs=[pl.BlockSpec(memory_space=pltpu.SMEM), qspec, prev, own, nxt, prev, own, nxt,
                  _resident((1, ATT_WIDTH))],
        out_specs=qspec,
        compiler_params=_params("parallel", "parallel"),
        name="attn",
    )(sink, q3, k3, k3, k3, v3, v3, v3, wn)
    return out.reshape(TOKENS, ATT_WIDTH)


def _hgrn_plan(reverse):
    n = HG_CHUNK // HG_SUB
    order = range(n - 1, -1, -1) if reverse else range(n)
    blocks = [slice(i * HG_SUB, (i + 1) * HG_SUB) for i in order]
    exits = [blk.start if reverse else blk.stop - 1 for blk in blocks]
    mids = [blk.start + HG_SUB // 2 - (0 if reverse else 1) for blk in blocks]
    return blocks, exits, mids


def _assemble(parts):
    by_start = {blk.start: rows for blk, rows in parts}
    like = parts[0][1]
    zero = jnp.zeros((HG_SUB, like.shape[1]), like.dtype)
    return jnp.concatenate([by_start.get(lo, zero) for lo in range(0, HG_CHUNK, HG_SUB)], axis=0)


def _nt(a, b):
    return _dot(a, b.T)


def _hgrn_chunk(lq, lk, v, refs, diag_mask, reverse):
    blocks, _, _ = _hgrn_plan(reverse)
    n = len(blocks)
    x = [refs[k:k + 1, :] for k in range(n)]
    m = [refs[n + k:n + k + 1, :] for k in range(n)]
    entry = [jnp.zeros_like(x[0])] + x[:-1]

    def scaled(a, k, log2_factor):
        if log2_factor is None:
            return a[blocks[k]]
        return a[blocks[k]] * jnp.exp2(log2_factor).astype(BF16)

    q_in = _assemble([(blocks[k], scaled(lq, k, entry[k] if k else None)) for k in range(n)])
    k_out = _assemble([(blocks[k], scaled(lk, k, x[n - 1] - x[k] if k < n - 1 else None))
                       for k in range(n)])
    q_d = _assemble([(blocks[k], scaled(lq, k, entry[k] - m[k])) for k in range(n)])
    lhs = [_assemble([(blocks[kq], scaled(lq, kq, entry[kq] - x[kj] if kq > kj + 1 else None))
                      for kq in range(kj + 1, n)])
           for kj in range(n - 1)]
    lk32 = lk.astype(F32)
    k_d32 = _assemble([(blocks[k], lk32[blocks[k]] * jnp.exp2(m[k] - x[k])) for k in range(n)])
    key = lax.broadcasted_iota(jnp.int32, (1, HG_CHUNK), 1)
    in_block = [(key >= blocks[kj].start) & (key < blocks[kj].stop) for kj in range(n - 1)]
    zero = jnp.zeros((), BF16)
    decay = jnp.exp2(x[n - 1])
    heads = [slice(h * HG_DIM, (h + 1) * HG_DIM) for h in range(HG_HEADS)]
    lhs_cat = [jnp.concatenate([a[:, sl] for a in lhs], axis=1) for sl in heads]
    rhs_t, k_dt = [], []
    for sl in heads:
        lk_t = lk32[:, sl].T.astype(BF16)
        rhs_t.append(jnp.concatenate([jnp.where(mask, lk_t, zero) for mask in in_block], axis=0))
        k_dt.append(k_d32[:, sl].T.astype(BF16))

    def matmuls(st_ref):
        scores = [_dot(a, b) for a, b in zip(lhs_cat, rhs_t)]
        diag = [_dot(q_d[:, sl], b) for sl, b in zip(heads, k_dt)]
        states = [st_ref[h] for h in range(HG_HEADS)]
        inter = [_dot(q_in[:, sl], st.T.astype(BF16)) for sl, st in zip(heads, states)]
        grow = [lax.dot_general(v[:, sl], k_out[:, sl], TN_DIMS, preferred_element_type=F32)
                for sl in heads]
        for h, (sl, st, g) in enumerate(zip(heads, states, grow)):
            st_ref[h] = st * decay[:, sl] + g

        def finish(o_ref, rows):
            probs = [(s + jnp.where(diag_mask, d, 0.0)).astype(BF16)
                     for s, d in zip(scores, diag)]
            outs = [_dot(p, v[:, sl]) + i for p, sl, i in zip(probs, heads, inter)]
            o_ref[0, rows, :] = jnp.concatenate(outs, axis=1).astype(BF16)

        return finish

    return matmuls


def _hgrn_body(lqf_ref, lkf_ref, rf_ref, hvf_ref, lqb_ref, lkb_ref, rb_ref, hvb_ref,
               of_ref, ob_ref, stf_ref, stb_ref):
    @pl.when(pl.program_id(1) == 0)
    def _():
        stf_ref[...] = jnp.zeros_like(stf_ref)
        stb_ref[...] = jnp.zeros_like(stb_ref)

    t = lax.broadcasted_iota(jnp.int32, (HG_CHUNK, HG_CHUNK), 0)
    s = lax.broadcasted_iota(jnp.int32, (HG_CHUNK, HG_CHUNK), 1)
    same_sub = (t // HG_SUB) == (s // HG_SUB)
    chunks = HG_TILE // HG_CHUNK
    directions = (
        (lqf_ref, lkf_ref, rf_ref, hvf_ref, of_ref, stf_ref, same_sub & (s <= t), False),
        (lqb_ref, lkb_ref, rb_ref, hvb_ref, ob_ref, stb_ref, same_sub & (s >= t), True),
    )
    units = [(c, d) for c in range(chunks) for d in directions]
    prepared, finishing = None, None
    for unit in units + [None, None]:
        upcoming = None
        if unit is not None:
            c, (lq_ref, lk_ref, r_ref, hv_ref, o_ref, st_ref, mask, reverse) = unit
            idx = chunks - 1 - c if reverse else c
            rows = slice(idx * HG_CHUNK, (idx + 1) * HG_CHUNK)
            matmuls = _hgrn_chunk(lq_ref[0, rows, :], lk_ref[0, rows, :], hv_ref[0, rows, :],
                                  r_ref[0, idx * HG_REFS:(idx + 1) * HG_REFS, :], mask, reverse)
            upcoming = (matmuls, st_ref, o_ref, rows)
        started = None
        if prepared is not None:
            matmuls, st_ref, o_ref, rows = prepared
            started = functools.partial(matmuls(st_ref), o_ref, rows)
        if finishing is not None:
            finishing()
        prepared, finishing = upcoming, started


def _hgrn(hv, lqf, lkf, rf, lqb, lkb, rb):
    tiles = SEQ // HG_TILE
    ref_rows = HG_TILE // HG_CHUNK * HG_REFS
    tok3 = lambda a: a.reshape(BATCH, SEQ, HG_WIDTH)
    ref3 = lambda a: a.reshape(BATCH, SEQ // HG_CHUNK * HG_REFS, HG_WIDTH)
    fwd, bwd = (lambda b, i: (b, i, 0)), (lambda b, i: (b, tiles - 1 - i, 0))
    tok = lambda index: pl.BlockSpec((1, HG_TILE, HG_WIDTH), index)
    ref = lambda index: pl.BlockSpec((1, ref_rows, HG_WIDTH), index)
    state = pltpu.VMEM((HG_HEADS, HG_DIM, HG_DIM), F32)
    of, ob = pl.pallas_call(
        _hgrn_body,
        out_shape=[jax.ShapeDtypeStruct((BATCH, SEQ, HG_WIDTH), BF16)] * 2,
        grid=(BATCH, tiles),
        in_specs=[tok(fwd), tok(fwd), ref(fwd), tok(fwd), tok(bwd), tok(bwd), ref(bwd), tok(bwd)],
        out_specs=[tok(fwd), tok(bwd)],
        scratch_shapes=[state, state],
        compiler_params=_params("parallel", "arbitrary"),
        name="hgrn",
    )(tok3(lqf), tok3(lkf), ref3(rf), tok3(hv), tok3(lqb), tok3(lkb), ref3(rb), tok3(hv))
    return of.reshape(TOKENS, HG_WIDTH), ob.reshape(TOKENS, HG_WIDTH)


def _outproj_body(x_ref, att_ref, of_ref, ob_ref, gate_ref, wh_ref, wo_ref, post_ref, o_ref):
    o = of_ref[...].astype(F32) + ob_ref[...].astype(F32)
    gate = gate_ref[...].astype(F32)
    wh = wh_ref[...]
    heads = []
    for h in range(HG_HEADS):
        sl = slice(h * HG_DIM, (h + 1) * HG_DIM)
        heads.append((_rms(o[:, sl], wh) * gate[:, sl]).astype(BF16))
    y = _dot(att_ref[...], wo_ref[:ATT_WIDTH, :])
    for h in range(HG_HEADS):
        lo = ATT_WIDTH + h * HG_DIM
        y = y + _dot(heads[h], wo_ref[lo:lo + HG_DIM, :])
    o_ref[...] = x_ref[...] + _rms(y, post_ref[...])


def _outproj(x, att, of, ob, gate, wh, wo, post):
    row = lambda width: pl.BlockSpec((ROW_TILE, width), lambda i: (i, 0))
    return pl.pallas_call(
        _outproj_body,
        out_shape=jax.ShapeDtypeStruct((TOKENS, D_MODEL), F32),
        grid=(TOKENS // ROW_TILE,),
        in_specs=[row(D_MODEL), row(ATT_WIDTH), row(HG_WIDTH), row(HG_WIDTH), row(HG_WIDTH),
                  _resident((1, HG_DIM)), _resident((ATT_WIDTH + HG_WIDTH, D_MODEL)),
                  _resident((1, D_MODEL))],
        out_specs=row(D_MODEL),
        compiler_params=_params("parallel"),
        name="outproj",
    )(x, att, of, ob, gate, wh, wo, post)


def kernel(x, ffn1_pre_norm, ffn1_post_norm, ffn1_w_gate, ffn1_w_up, ffn1_w_down, mix_pre_norm, mix_post_norm, w_in, attn_sink, attn_out_norm, hgrn_lb_fwd, hgrn_lb_bwd, hgrn_out_norm, w_out, ffn2_pre_norm, ffn2_post_norm, ffn2_w_gate, ffn2_w_up, ffn2_w_down):
    assert x.shape == (BATCH, SEQ, D_MODEL) and ffn1_pre_norm.shape[0] == 1
    h = x.reshape(TOKENS, D_MODEL)
    bf = lambda w: w[0].astype(BF16)
    h = _ffn(h, ffn1_pre_norm, ffn1_post_norm, bf(ffn1_w_gate), bf(ffn1_w_up), bf(ffn1_w_down))
    q, k2, v2, lqf, lkf, rf, lqb, lkb, rb, hv, gate = _inproj(
        h, mix_pre_norm, bf(w_in), hgrn_lb_fwd, hgrn_lb_bwd)
    att = _attn(attn_sink[0], q, k2, v2, attn_out_norm)
    of, ob = _hgrn(hv, lqf, lkf, rf, lqb, lkb, rb)
    h = _outproj(h, att, of, ob, gate, hgrn_out_norm, bf(w_out), mix_post_norm)
    h = _ffn(h, ffn2_pre_norm, ffn2_post_norm, bf(ffn2_w_gate), bf(ffn2_w_up), bf(ffn2_w_down))
    return h.reshape(BATCH, SEQ, D_MODEL)
```

```python
import functools
import math

import jax
import jax.numpy as jnp
from jax import lax
from jax.experimental import pallas as pl
from jax.experimental.pallas import tpu as pltpu

D_MODEL = 1024
BATCH = 16
SEQ = 4096
ATT_HEADS = 8
ATT_KV_HEADS = 2
ATT_HEAD_DIM = 64
ROT_DIM = ATT_HEAD_DIM // 4
ROPE_THETA = 500000.0
WINDOW = 128
HG_HEADS = 4
HG_DIM = 128
ATT_WIDTH = ATT_HEADS * ATT_HEAD_DIM
KV_WIDTH = ATT_KV_HEADS * ATT_HEAD_DIM
HG_WIDTH = HG_HEADS * HG_DIM
IN_WIDTH = ATT_WIDTH + 2 * KV_WIDTH + 5 * HG_WIDTH
D_FF = 2816
FFN_RES_WEIGHT = 0.5
EPS = 1e-6
NEG_INF = -1e30

TOKENS = BATCH * SEQ
LANES = 128
ROW_TILE = 512
MXU_WIDTH = 256
FF_CHUNK = MXU_WIDTH
ATT_BLOCK = WINDOW
ATT_TILE = 512
KV_SLABS = 2 * ATT_KV_HEADS
HG_CHUNK = 128
HG_SUB = 32
HG_REFS = 2 * (HG_CHUNK // HG_SUB)
LOG2E = math.log2(math.e)
HG_TILE = 512
VMEM_LIMIT = 48 * 1024 * 1024
FUSED_VMEM_LIMIT = 60 * 1024 * 1024

F32 = jnp.float32
BF16 = jnp.bfloat16
NT_DIMS = (((1,), (1,)), ((), ()))
TN_DIMS = (((0,), (0,)), ((), ()))


def _rms(x, w):
    return x * lax.rsqrt(jnp.mean(x * x, axis=-1, keepdims=True) + EPS) * w


def _dot(a, b):
    return jnp.dot(a, b, preferred_element_type=F32)


def _resident(shape):
    return pl.BlockSpec(shape, lambda *_: (0,) * len(shape), pipeline_mode=pl.Buffered(1))


def _params(*semantics, vmem_limit=VMEM_LIMIT):
    return pltpu.CompilerParams(dimension_semantics=semantics, vmem_limit_bytes=vmem_limit)


N_TILES = TOKENS // ROW_TILE


def _lead_tile(i):
    return jnp.minimum(i, N_TILES - 1)


def _lag_tile(i):
    return jnp.maximum(i - 1, 0)


def _row_spec(width, tile):
    return pl.BlockSpec((ROW_TILE, width), lambda i: (tile(i), 0))


def _ffn_compute(x, pre_ref, post_ref, wg_ref, wu_ref, wd_ref):
    h = _rms(x, pre_ref[...]).astype(BF16)
    acc = None
    for c in range(D_FF // FF_CHUNK):
        sl = slice(c * FF_CHUNK, (c + 1) * FF_CHUNK)
        g = _dot(h, wg_ref[:, sl])
        u = _dot(h, wu_ref[:, sl])
        a = (g * jax.nn.sigmoid(g) * u).astype(BF16)
        d = _dot(a, wd_ref[sl, :])
        acc = d if acc is None else acc + d
    return x + FFN_RES_WEIGHT * _rms(acc, post_ref[...])


def _ffn_weight_specs():
    return [_resident((1, D_MODEL)), _resident((1, D_MODEL)), _resident((D_MODEL, D_FF)),
            _resident((D_MODEL, D_FF)), _resident((D_FF, D_MODEL))]


def _lower_bound(lb_ref, layer):
    rows = [lb_ref[r:r + 1, :] for r in range(lb_ref.shape[0])]
    m = functools.reduce(jnp.maximum, rows)
    e = [jnp.exp(r - m) for r in rows]
    return sum(e[:layer + 1]) / sum(e)


def _rope(t, cos, sin_hi, sin_lo):
    return (t * cos + pltpu.roll(t, LANES - ROT_DIM // 2, 1) * sin_hi
            + pltpu.roll(t, ROT_DIM // 2, 1) * sin_lo)


def _inproj_compute(u, w_ref, lbf_ref, lbb_ref, cos_ref, shi_ref, slo_ref,
                    q_ref, k4_ref, v4_ref, lqf_ref, lkf_ref, rf_ref, lqb_ref, lkb_ref, rb_ref,
                    hv_ref, gate_ref):
    cos, shi, slo = cos_ref[...], shi_ref[...], slo_ref[...]

    def proj(lo, width):
        return _dot(u, w_ref[:, lo:lo + width])

    scale = ATT_HEAD_DIM ** -0.5 * LOG2E
    for c in range(ATT_WIDTH // MXU_WIDTH):
        q2 = proj(c * MXU_WIDTH, MXU_WIDTH)
        for s in range(MXU_WIDTH // LANES):
            lo = c * MXU_WIDTH + s * LANES
            q = _rope(q2[:, s * LANES:(s + 1) * LANES], cos, shi, slo)
            q_ref[:, lo:lo + LANES] = (q * scale).astype(BF16)
    kv = proj(ATT_WIDTH, 2 * KV_WIDTH)
    left = lax.broadcasted_iota(jnp.int32, (1, LANES), 1) < ATT_HEAD_DIM
    for t, t4_ref in ((_rope(kv[:, :KV_WIDTH], cos, shi, slo), k4_ref),
                      (kv[:, KV_WIDTH:], v4_ref)):
        swapped = pltpu.roll(t, LANES // 2, 1)
        slabs = (jnp.where(left, t, 0.0), jnp.where(left, 0.0, swapped),
                 jnp.where(left, swapped, 0.0), jnp.where(left, 0.0, t))
        for i, slab in enumerate(slabs):
            t4_ref[:, i * LANES:(i + 1) * LANES] = slab.astype(BF16)
    off = ATT_WIDTH + 2 * KV_WIDTH
    qs = jax.nn.sigmoid(proj(off, HG_WIDTH))
    off += HG_WIDTH
    t = lax.broadcasted_iota(jnp.int32, (HG_CHUNK, HG_CHUNK), 0)
    s = lax.broadcasted_iota(jnp.int32, (HG_CHUNK, HG_CHUNK), 1)
    prefix = jnp.where(s <= t, 1.0, 0.0).astype(BF16)
    suffix = jnp.where(s >= t, 1.0, 0.0).astype(BF16)
    for lb_ref, lq_ref, lk_ref, ref_ref, cum, reverse in (
            (lbf_ref, lqf_ref, lkf_ref, rf_ref, prefix, False),
            (lbb_ref, lqb_ref, lkb_ref, rb_ref, suffix, True)):
        lb = _lower_bound(lb_ref, 0)
        f = lb + (1.0 - lb) * jax.nn.sigmoid(proj(off, HG_WIDTH))
        g = (jnp.log(f) * LOG2E).astype(BF16)
        kk = 1.0 - f
        blocks, exits, mids = _hgrn_plan(reverse)
        for c in range(ROW_TILE // HG_CHUNK):
            lo = c * HG_CHUNK
            b = _dot(cum, g[lo:lo + HG_CHUNK])
            x = [b[r:r + 1, :] for r in exits]
            ref_ref[c * HG_REFS:(c + 1) * HG_REFS, :] = jnp.concatenate(
                x + [b[r:r + 1, :] for r in mids], axis=0)
            for k, blk in enumerate(blocks):
                rows = slice(lo + blk.start, lo + blk.stop)
                entry = b[blk] - x[k - 1] if k else b[blk]
                lq_ref[rows, :] = (qs[rows] * jnp.exp2(entry)).astype(BF16)
                lk_ref[rows, :] = (kk[rows] * jnp.exp2(x[k] - b[blk])).astype(BF16)
        off += HG_WIDTH
    hi = proj(off, HG_WIDTH)
    hv_ref[...] = (hi * jax.nn.sigmoid(hi)).astype(BF16)
    off += HG_WIDTH
    hg = proj(off, HG_WIDTH)
    gate_ref[...] = (hg * jax.nn.sigmoid(hg)).astype(BF16)


def _rope_tables():
    half = ROT_DIM // 2
    inv_freq = jnp.exp(-math.log(ROPE_THETA) * jnp.arange(half, dtype=F32) * 2.0 / ROT_DIM)
    ang = jnp.arange(SEQ, dtype=F32)[:, None] * inv_freq[None, :]
    cos, sin = jnp.cos(ang), jnp.sin(ang)
    rest = ATT_HEAD_DIM - ROT_DIM
    ones, zeros = jnp.ones((SEQ, rest), F32), jnp.zeros((SEQ, rest), F32)
    zhalf = jnp.zeros((SEQ, half), F32)
    cos_t = jnp.concatenate([cos, cos, ones], axis=1)
    sin_hi = jnp.concatenate([-sin, zhalf, zeros], axis=1)
    sin_lo = jnp.concatenate([zhalf, sin, zeros], axis=1)
    return tuple(jnp.concatenate([t, t], axis=1) for t in (cos_t, sin_hi, sin_lo))


def _ffn_inproj_body(x_ref, pre1_ref, post1_ref, wg_ref, wu_ref, wd_ref, prem_ref, *rest):
    inproj_refs, x1_ref, u_ref = rest[:-2], rest[-2], rest[-1]

    @pl.when(pl.program_id(0) == 0)
    def _():
        u_ref[...] = jnp.zeros_like(u_ref)

    _inproj_compute(u_ref[...], *inproj_refs)
    x1 = _ffn_compute(x_ref[...], pre1_ref, post1_ref, wg_ref, wu_ref, wd_ref)
    x1_ref[...] = x1
    u_ref[...] = _rms(x1, prem_ref[...]).astype(BF16)


def _ffn_inproj(x, pre1, post1, wg, wu, wd, pre_m, w_in, lb_fwd, lb_bwd):
    tiles_per_seq = SEQ // ROW_TILE
    table = pl.BlockSpec((ROW_TILE, LANES), lambda i: (_lag_tile(i) % tiles_per_seq, 0))
    ref_rows = ROW_TILE // HG_CHUNK * HG_REFS
    refs = pl.BlockSpec((ref_rows, HG_WIDTH), lambda i: (_lag_tile(i), 0))
    tok = lambda width: (jax.ShapeDtypeStruct((TOKENS, width), BF16), _row_spec(width, _lag_tile))
    ref = (jax.ShapeDtypeStruct((TOKENS // HG_CHUNK * HG_REFS, HG_WIDTH), F32), refs)
    outs = (tok(ATT_WIDTH), tok(KV_SLABS * LANES), tok(KV_SLABS * LANES),
            tok(HG_WIDTH), tok(HG_WIDTH), ref, tok(HG_WIDTH), tok(HG_WIDTH), ref,
            tok(HG_WIDTH), tok(HG_WIDTH),
            (jax.ShapeDtypeStruct((TOKENS, D_MODEL), F32), _row_spec(D_MODEL, _lead_tile)))
    *mixer_inputs, x1 = pl.pallas_call(
        _ffn_inproj_body,
        out_shape=[shape for shape, _ in outs],
        grid=(N_TILES + 1,),
        in_specs=[_row_spec(D_MODEL, _lead_tile)] + _ffn_weight_specs() + [
            _resident((1, D_MODEL)), _resident((D_MODEL, IN_WIDTH)),
            _resident(lb_fwd.shape), _resident(lb_bwd.shape), table, table, table],
        out_specs=[spec for _, spec in outs],
        scratch_shapes=[pltpu.VMEM((ROW_TILE, D_MODEL), BF16)],
        compiler_params=_params("arbitrary", vmem_limit=FUSED_VMEM_LIMIT),
        name="ffn_inproj",
    )(x, pre1, post1, wg, wu, wd, pre_m, w_in, lb_fwd, lb_bwd, *_rope_tables())
    return x1, mixer_inputs


def _attn_body(sink_ref, q_ref, kp_ref, kc_ref, kn_ref, vp_ref, vc_ref, vn_ref, wn_ref, o_ref):
    tile = pl.program_id(1)
    blocks_per_tile = ATT_TILE // ATT_BLOCK
    last_block = SEQ // ATT_BLOCK - 1
    kj = lax.broadcasted_iota(jnp.int32, (ATT_BLOCK, 2 * ATT_BLOCK), 0)
    qi = lax.broadcasted_iota(jnp.int32, (ATT_BLOCK, 2 * ATT_BLOCK), 1) % ATT_BLOCK
    left = lax.broadcasted_iota(jnp.int32, (1, LANES), 1) < ATT_HEAD_DIM
    upper = lax.broadcasted_iota(jnp.int32, (LANES, 1), 0) < ATT_HEAD_DIM
    first_slab = lax.broadcasted_iota(jnp.int32, (1, 2 * ATT_BLOCK), 1) < ATT_BLOCK
    wn = wn_ref[...]

    def piece(p_ref, c_ref, n_ref, j, which):
        idx = j + which - 1
        if idx < 0:
            return p_ref[0]
        if idx >= blocks_per_tile:
            return n_ref[0]
        return c_ref[0, idx * ATT_BLOCK:(idx + 1) * ATT_BLOCK, :]

    def split(pieces, g):
        a = [p[:, 2 * g * LANES:(2 * g + 1) * LANES] for p in pieces]
        b = [p[:, (2 * g + 1) * LANES:(2 * g + 2) * LANES] for p in pieces]
        return jnp.concatenate(a + b, axis=0)

    band = 3 * ATT_BLOCK
    in_window = (kj >= qi, kj <= qi)

    def start(j, g):
        block = tile * blocks_per_tile + j
        rows = slice(j * ATT_BLOCK, (j + 1) * ATT_BLOCK)
        k_ab = split([piece(kp_ref, kc_ref, kn_ref, j, w) for w in range(3)], g)
        v_ab = split([piece(vp_ref, vc_ref, vn_ref, j, w) for w in range(3)], g)
        slabs = [slice((2 * g + p) * LANES, (2 * g + p + 1) * LANES) for p in range(2)]
        q2 = jnp.concatenate([q_ref[0, rows, sl] for sl in slabs], axis=0)
        s = lax.dot_general(k_ab, q2, NT_DIMS, preferred_element_type=F32)

        def finish():
            prev_ok, next_ok = in_window
            if j == 0:
                prev_ok = kj >= qi + jnp.where(block > 0, 0, ATT_BLOCK)
            if j == blocks_per_tile - 1:
                next_ok = kj <= qi - jnp.where(block < last_block, 0, ATT_BLOCK)
            probs, inv = [], []
            for side in range(2):
                heads = [2 * (2 * g + p) + side for p in range(2)]
                sink = jnp.where(first_slab, sink_ref[heads[0]], sink_ref[heads[1]]) * LOG2E
                lo = side * band
                sh = jnp.concatenate(
                    [jnp.where(prev_ok, s[lo:lo + ATT_BLOCK], NEG_INF),
                     s[lo + ATT_BLOCK:lo + 2 * ATT_BLOCK],
                     jnp.where(next_ok, s[lo + 2 * ATT_BLOCK:lo + band], NEG_INF)], axis=0)
                m = jnp.maximum(jnp.max(sh, axis=0, keepdims=True), sink)
                p = jnp.exp2(sh - m)
                inv.append(1.0 / (jnp.sum(p, axis=0, keepdims=True) + jnp.exp2(sink - m)))
                probs.append(p.astype(BF16))
            o_t = lax.dot_general(v_ab, jnp.concatenate(probs, axis=0), TN_DIMS,
                                  preferred_element_type=F32)
            o_t = o_t * jnp.where(upper, inv[0], inv[1])
            return [o_t[:, p * ATT_BLOCK:(p + 1) * ATT_BLOCK].T for p in range(2)]

        return finish

    units = [(j, g) for j in range(blocks_per_tile) for g in range(ATT_KV_HEADS)]
    finishing, outs = None, []
    for i, unit in enumerate(units + [None]):
        started = start(*unit) if unit is not None else None
        if finishing is not None:
            outs.extend(finishing())
        finishing = started
        if i and i % ATT_KV_HEADS == 0:
            j = i // ATT_KV_HEADS - 1
            att = _rms(jnp.concatenate(outs, axis=1), wn)
            o_ref[0, j * ATT_BLOCK:(j + 1) * ATT_BLOCK, :] = att.astype(BF16)
            outs = []


def _attn(sink, q, k4, v4, wn):
    width = KV_SLABS * LANES
    q3 = q.reshape(BATCH, SEQ, ATT_WIDTH)
    k3 = k4.reshape(BATCH, SEQ, width)
    v3 = v4.reshape(BATCH, SEQ, width)
    ratio = ATT_TILE // ATT_BLOCK
    last = SEQ // ATT_BLOCK - 1
    own = pl.BlockSpec((1, ATT_TILE, width), lambda b, i: (b, i, 0))
    prev = pl.BlockSpec((1, ATT_BLOCK, width),
                        lambda b, i: (b, jnp.maximum(i * ratio - 1, 0), 0))
    nxt = pl.BlockSpec((1, ATT_BLOCK, width),
                       lambda b, i: (b, jnp.minimum((i + 1) * ratio, last), 0))
    qspec = pl.BlockSpec((1, ATT_TILE, ATT_WIDTH), lambda b, i: (b, i, 0))
    out = pl.pallas_call(
        _attn_body,
        out_shape=jax.ShapeDtypeStruct((BATCH, SEQ, ATT_WIDTH), BF16),
        grid=(BATCH, SEQ // ATT_TILE),
        in_specs=[pl.BlockSpec(memory_space=pltpu.SMEM), qspec, prev, own, nxt, prev, own, nxt,
                  _resident((1, ATT_WIDTH))],
        out_specs=qspec,
        compiler_params=_params("parallel", "parallel"),
        name="attn",
    )(sink, q3, k3, k3, k3, v3, v3, v3, wn)
    return out.reshape(TOKENS, ATT_WIDTH)


def _hgrn_plan(reverse):
    n = HG_CHUNK // HG_SUB
    order = range(n - 1, -1, -1) if reverse else range(n)
    blocks = [slice(i * HG_SUB, (i + 1) * HG_SUB) for i in order]
    exits = [blk.start if reverse else blk.stop - 1 for blk in blocks]
    mids = [blk.start + HG_SUB // 2 - (0 if reverse else 1) for blk in blocks]
    return blocks, exits, mids


def _assemble(parts):
    by_start = {blk.start: rows for blk, rows in parts}
    like = parts[0][1]
    zero = jnp.zeros((HG_SUB, like.shape[1]), like.dtype)
    return jnp.concatenate([by_start.get(lo, zero) for lo in range(0, HG_CHUNK, HG_SUB)], axis=0)


def _nt(a, b):
    return _dot(a, b.T)


def _hgrn_chunk(lq, lk, v, refs, diag_mask, reverse):
    blocks, _, _ = _hgrn_plan(reverse)
    n = len(blocks)
    x = [refs[k:k + 1, :] for k in range(n)]
    m = [refs[n + k:n + k + 1, :] for k in range(n)]
    entry = [jnp.zeros_like(x[0])] + x[:-1]

    def scaled(a, k, log2_factor):
        if log2_factor is None:
            return a[blocks[k]]
        return a[blocks[k]] * jnp.exp2(log2_factor).astype(BF16)

    q_in = _assemble([(blocks[k], scaled(lq, k, entry[k] if k else None)) for k in range(n)])
    k_out = _assemble([(blocks[k], scaled(lk, k, x[n - 1] - x[k] if k < n - 1 else None))
                       for k in range(n)])
    q_d = _assemble([(blocks[k], scaled(lq, k, entry[k] - m[k])) for k in range(n)])
    lhs = [_assemble([(blocks[kq], scaled(lq, kq, entry[kq] - x[kj] if kq > kj + 1 else None))
                      for kq in range(kj + 1, n)])
           for kj in range(n - 1)]
    lk32 = lk.astype(F32)
    k_d32 = _assemble([(blocks[k], lk32[blocks[k]] * jnp.exp2(m[k] - x[k])) for k in range(n)])
    key = lax.broadcasted_iota(jnp.int32, (1, HG_CHUNK), 1)
    in_block = [(key >= blocks[kj].start) & (key < blocks[kj].stop) for kj in range(n - 1)]
    zero = jnp.zeros((), BF16)
    decay = jnp.exp2(x[n - 1])
    heads = [slice(h * HG_DIM, (h + 1) * HG_DIM) for h in range(HG_HEADS)]
    lhs_cat = [jnp.concatenate([a[:, sl] for a in lhs], axis=1) for sl in heads]
    rhs_t, k_dt = [], []
    for sl in heads:
        lk_t = lk32[:, sl].T.astype(BF16)
        rhs_t.append(jnp.concatenate([jnp.where(mask, lk_t, zero) for mask in in_block], axis=0))
        k_dt.append(k_d32[:, sl].T.astype(BF16))

    def matmuls(st_ref):
        scores = [_dot(a, b) for a, b in zip(lhs_cat, rhs_t)]
        diag = [_dot(q_d[:, sl], b) for sl, b in zip(heads, k_dt)]
        states = [st_ref[h] for h in range(HG_HEADS)]
        inter = [_dot(q_in[:, sl], st.T.astype(BF16)) for sl, st in zip(heads, states)]
        grow = [lax.dot_general(v[:, sl], k_out[:, sl], TN_DIMS, preferred_element_type=F32)
                for sl in heads]
        for h, (sl, st, g) in enumerate(zip(heads, states, grow)):
            st_ref[h] = st * decay[:, sl] + g

        def finish(o_ref, rows):
            probs = [(s + jnp.where(diag_mask, d, 0.0)).astype(BF16)
                     for s, d in zip(scores, diag)]
            outs = [_dot(p, v[:, sl]) + i for p, sl, i in zip(probs, heads, inter)]
            o_ref[0, rows, :] = jnp.concatenate(outs, axis=1).astype(BF16)

        return finish

    return matmuls


def _hgrn_body(lqf_ref, lkf_ref, rf_ref, hvf_ref, lqb_ref, lkb_ref, rb_ref, hvb_ref,
               of_ref, ob_ref, stf_ref, stb_ref):
    @pl.when(pl.program_id(1) == 0)
    def _():
        stf_ref[...] = jnp.zeros_like(stf_ref)
        stb_ref[...] = jnp.zeros_like(stb_ref)

    t = lax.broadcasted_iota(jnp.int32, (HG_CHUNK, HG_CHUNK), 0)
    s = lax.broadcasted_iota(jnp.int32, (HG_CHUNK, HG_CHUNK), 1)
    same_sub = (t // HG_SUB) == (s // HG_SUB)
    chunks = HG_TILE // HG_CHUNK
    directions = (
        (lqf_ref, lkf_ref, rf_ref, hvf_ref, of_ref, stf_ref, same_sub & (s <= t), False),
        (lqb_ref, lkb_ref, rb_ref, hvb_ref, ob_ref, stb_ref, same_sub & (s >= t), True),
    )
    units = [(c, d) for c in range(chunks) for d in directions]
    prepared, finishing = None, None
    for unit in units + [None, None]:
        upcoming = None
        if unit is not None:
            c, (lq_ref, lk_ref, r_ref, hv_ref, o_ref, st_ref, mask, reverse) = unit
            idx = chunks - 1 - c if reverse else c
            rows = slice(idx * HG_CHUNK, (idx + 1) * HG_CHUNK)
            matmuls = _hgrn_chunk(lq_ref[0, rows, :], lk_ref[0, rows, :], hv_ref[0, rows, :],
                                  r_ref[0, idx * HG_REFS:(idx + 1) * HG_REFS, :], mask, reverse)
            upcoming = (matmuls, st_ref, o_ref, rows)
        started = None
        if prepared is not None:
            matmuls, st_ref, o_ref, rows = prepared
            started = functools.partial(matmuls(st_ref), o_ref, rows)
        if finishing is not None:
            finishing()
        prepared, finishing = upcoming, started


def _hgrn(hv, lqf, lkf, rf, lqb, lkb, rb):
    tiles = SEQ // HG_TILE
    ref_rows = HG_TILE // HG_CHUNK * HG_REFS
    tok3 = lambda a: a.reshape(BATCH, SEQ, HG_WIDTH)
    ref3 = lambda a: a.reshape(BATCH, SEQ // HG_CHUNK * HG_REFS, HG_WIDTH)
    fwd, bwd = (lambda b, i: (b, i, 0)), (lambda b, i: (b, tiles - 1 - i, 0))
    tok = lambda index: pl.BlockSpec((1, HG_TILE, HG_WIDTH), index)
    ref = lambda index: pl.BlockSpec((1, ref_rows, HG_WIDTH), index)
    state = pltpu.VMEM((HG_HEADS, HG_DIM, HG_DIM), F32)
    of, ob = pl.pallas_call(
        _hgrn_body,
        out_shape=[jax.ShapeDtypeStruct((BATCH, SEQ, HG_WIDTH), BF16)] * 2,
        grid=(BATCH, tiles),
        in_specs=[tok(fwd), tok(fwd), ref(fwd), tok(fwd), tok(bwd), tok(bwd), ref(bwd), tok(bwd)],
        out_specs=[tok(fwd), tok(bwd)],
        scratch_shapes=[state, state],
        compiler_params=_params("parallel", "arbitrary"),
        name="hgrn",
    )(tok3(lqf), tok3(lkf), ref3(rf), tok3(hv), tok3(lqb), tok3(lkb), ref3(rb), tok3(hv))
    return of.reshape(TOKENS, HG_WIDTH), ob.reshape(TOKENS, HG_WIDTH)


def _outproj_compute(x, att, of_ref, ob_ref, gate_ref, wh_ref, wo_ref, post_ref):
    o = of_ref[...].astype(F32) + ob_ref[...].astype(F32)
    gate = gate_ref[...].astype(F32)
    wh = wh_ref[...]
    heads = [slice(h * HG_DIM, (h + 1) * HG_DIM) for h in range(HG_HEADS)]
    hgo = jnp.concatenate([(_rms(o[:, sl], wh) * gate[:, sl]).astype(BF16) for sl in heads],
                          axis=1)
    y = _dot(att, wo_ref[:ATT_WIDTH, :]) + _dot(hgo, wo_ref[ATT_WIDTH:, :])
    return x + _rms(y, post_ref[...])


def _outproj_ffn_body(x_ref, att_ref, of_ref, ob_ref, gate_ref, wh_ref, wo_ref, postm_ref,
                      pre2_ref, post2_ref, wg_ref, wu_ref, wd_ref, o_ref, x2_ref):
    @pl.when(pl.program_id(0) == 0)
    def _():
        x2_ref[...] = jnp.zeros_like(x2_ref)

    x2_lag = x2_ref[...]
    x2 = _outproj_compute(x_ref[...], att_ref[...], of_ref, ob_ref, gate_ref, wh_ref, wo_ref,
                          postm_ref)
    o_ref[...] = _ffn_compute(x2_lag, pre2_ref, post2_ref, wg_ref, wu_ref, wd_ref)
    x2_ref[...] = x2


def _outproj_ffn(x1, att, of, ob, gate, wh, wo, post_m, pre2, post2, wg, wu, wd):
    lead = lambda width: _row_spec(width, _lead_tile)
    return pl.pallas_call(
        _outproj_ffn_body,
        out_shape=jax.ShapeDtypeStruct((TOKENS, D_MODEL), F32),
        grid=(N_TILES + 1,),
        in_specs=[lead(D_MODEL), lead(ATT_WIDTH), lead(HG_WIDTH), lead(HG_WIDTH), lead(HG_WIDTH),
                  _resident((1, HG_DIM)), _resident((ATT_WIDTH + HG_WIDTH, D_MODEL)),
                  _resident((1, D_MODEL))] + _ffn_weight_specs(),
        out_specs=_row_spec(D_MODEL, _lag_tile),
        scratch_shapes=[pltpu.VMEM((ROW_TILE, D_MODEL), F32)],
        compiler_params=_params("arbitrary", vmem_limit=FUSED_VMEM_LIMIT),
        name="outproj_ffn",
    )(x1, att, of, ob, gate, wh, wo, post_m, pre2, post2, wg, wu, wd)


def kernel(x, ffn1_pre_norm, ffn1_post_norm, ffn1_w_gate, ffn1_w_up, ffn1_w_down, mix_pre_norm, mix_post_norm, w_in, attn_sink, attn_out_norm, hgrn_lb_fwd, hgrn_lb_bwd, hgrn_out_norm, w_out, ffn2_pre_norm, ffn2_post_norm, ffn2_w_gate, ffn2_w_up, ffn2_w_down):
    assert x.shape == (BATCH, SEQ, D_MODEL) and ffn1_pre_norm.shape[0] == 1
    h = x.reshape(TOKENS, D_MODEL)
    bf = lambda w: w[0].astype(BF16)
    h, (q, k4, v4, lqf, lkf, rf, lqb, lkb, rb, hv, gate) = _ffn_inproj(
        h, ffn1_pre_norm, ffn1_post_norm, bf(ffn1_w_gate), bf(ffn1_w_up), bf(ffn1_w_down),
        mix_pre_norm, bf(w_in), hgrn_lb_fwd, hgrn_lb_bwd)
    att = _attn(attn_sink[0], q, k4, v4, attn_out_norm)
    of, ob = _hgrn(hv, lqf, lkf, rf, lqb, lkb, rb)
    h = _outproj_ffn(h, att, of, ob, gate, hgrn_out_norm, bf(w_out), mix_post_norm,
                     ffn2_pre_norm, ffn2_post_norm, bf(ffn2_w_gate), bf(ffn2_w_up),
                     bf(ffn2_w_down))
    return h.reshape(BATCH, SEQ, D_MODEL)
```
